```python
import math
import jax, jax.numpy as jnp
from jax import lax
import numpy as np

D_MODEL = 1024
BATCH = 2
SEQ = 8192
DEPTH = 4

GRID_W = 64
CTX_LEN = 256

ATTN_HEADS = 4
ATTN_HEAD_DIM = 64
ATTN_V_DIM = 2 * ATTN_HEAD_DIM
ATTN_WIDTH = ATTN_HEADS * ATTN_V_DIM
QK_WIDTH = ATTN_HEADS * 2 * ATTN_HEAD_DIM
POOL_WINDOWS = (2, 4, 8, 16)
POOL_WIDTH = D_MODEL // 4
POOL_GROUP = POOL_WIDTH // len(POOL_WINDOWS)
CONV_WIDTH = D_MODEL // 4
CONV_KERNEL = 31
CONV_PAD = CONV_KERNEL // 2

MIX_WIDTH = ATTN_WIDTH + POOL_WIDTH + CONV_WIDTH
IN_WIDTH = 2 * QK_WIDTH + ATTN_WIDTH + POOL_WIDTH + 2 * CONV_WIDTH
SPLITS = (QK_WIDTH, 2 * QK_WIDTH, 2 * QK_WIDTH + ATTN_WIDTH,
          2 * QK_WIDTH + ATTN_WIDTH + POOL_WIDTH,
          2 * QK_WIDTH + ATTN_WIDTH + POOL_WIDTH + CONV_WIDTH)

FFN_HIDDEN = ((8 * D_MODEL + 3 * 256 - 1) // (3 * 256)) * 256
ROPE_THETA = 10000.0
EPS = 1e-6
BLOCK_Q = 128

kernel_name = "hybrid_diffattn_pool_conformer_dit"


def rms_norm(x, g):
    xf = x.astype(jnp.float32)
    y = xf * lax.rsqrt(jnp.mean(xf * xf, axis=-1, keepdims=True) + EPS)
    return (y * g.astype(jnp.float32)).astype(x.dtype)


def layer_norm(x, g, b):
    xf = x.astype(jnp.float32)
    mu = jnp.mean(xf, axis=-1, keepdims=True)
    var = jnp.mean(jnp.square(xf - mu), axis=-1, keepdims=True)
    y = (xf - mu) * lax.rsqrt(var + EPS)
    return (y * g.astype(jnp.float32) + b.astype(jnp.float32)).astype(x.dtype)


def adaln(cond, w_mod, b_mod):
    m = jnp.dot(jax.nn.silu(cond), w_mod) + b_mod
    return jnp.split(m[..., None, :], 6, axis=-1)


def modulate(h, shift, scale):
    return h * (1.0 + scale) + shift


def axial_rope_tables(n_rows):
    row = jnp.repeat(jnp.arange(n_rows), GRID_W).astype(jnp.float32)
    col = jnp.tile(jnp.arange(GRID_W), n_rows).astype(jnp.float32)
    half = ATTN_HEAD_DIM // 2
    inv_freq = ROPE_THETA ** (-jnp.arange(0, half, 2, dtype=jnp.float32) / half)
    ang = jnp.concatenate([row[:, None] * inv_freq, col[:, None] * inv_freq], axis=-1)
    return jnp.cos(ang), jnp.sin(ang)


def apply_rope(x, cos, sin):
    x1 = x[..., 0::2].astype(jnp.float32)
    x2 = x[..., 1::2].astype(jnp.float32)
    y1 = x1 * cos - x2 * sin
    y2 = x1 * sin + x2 * cos
    return jnp.stack([y1, y2], axis=-1).reshape(x.shape).astype(x.dtype)


def project_inputs(h, w_in, q_g, k_g):
    B, L, _ = h.shape
    proj = jnp.einsum('bld,de->ble', h, w_in)
    q, k, v, u_pool, a_conv, b_conv = jnp.split(proj, list(SPLITS), axis=-1)
    q = rms_norm(q.reshape(B, L, ATTN_HEADS, 2, ATTN_HEAD_DIM), q_g).transpose(0, 2, 3, 1, 4)
    k = rms_norm(k.reshape(B, L, ATTN_HEADS, 2, ATTN_HEAD_DIM), k_g).transpose(0, 2, 3, 1, 4)
    v = v.reshape(B, L, ATTN_HEADS, ATTN_V_DIM).transpose(0, 2, 1, 3)
    return q, k, v, u_pool, a_conv, b_conv


def diff_attention(q, k_all, v_all, lam):
    B, H, _, Lq, dh = q.shape
    nb = Lq // BLOCK_Q
    qb = q.reshape(B, H, 2, nb, BLOCK_Q, dh).transpose(3, 0, 1, 2, 4, 5)
    scale = dh ** -0.5

    def one_block(q_blk):
        s = jnp.einsum('bhcqd,bhckd->bhcqk', q_blk, k_all).astype(jnp.float32) * scale
        p = jax.nn.softmax(s, axis=-1)
        a = p[:, :, 0] - lam * p[:, :, 1]
        return jnp.einsum('bhqk,bhkd->bhqd', a.astype(v_all.dtype), v_all)

    o = lax.map(one_block, qb)
    return o.transpose(1, 2, 0, 3, 4).reshape(B, H, Lq, v_all.shape[-1])


def diff_attn_post(o, subln_g, lambda_init):
    B, H, L, dv = o.shape
    o = rms_norm(o, subln_g) * (1.0 - lambda_init)
    return o.transpose(0, 2, 1, 3).reshape(B, L, H * dv)


def multiscale_pool(u):
    B, L, C = u.shape
    uf = u.astype(jnp.float32)
    csum = jnp.concatenate([jnp.zeros((B, 1, C), jnp.float32), jnp.cumsum(uf, axis=1)], axis=1)
    t = jnp.arange(L)
    outs = []
    for gi, w in enumerate(POOL_WINDOWS):
        lo = jnp.clip(t - w // 2, 0, L)
        hi = jnp.clip(t + w - w // 2, 0, L)
        sl = slice(gi * POOL_GROUP, (gi + 1) * POOL_GROUP)
        seg = csum[:, :, sl]
        cnt = (hi - lo).astype(jnp.float32)[None, :, None]
        outs.append((seg[:, hi] - seg[:, lo]) / cnt - uf[:, :, sl])
    return jnp.concatenate(outs, axis=-1).astype(u.dtype)


def pool_mixer(u, pool_w, pool_scale):
    B, L, _ = u.shape
    p = multiscale_pool(u).reshape(B, L, len(POOL_WINDOWS), POOL_GROUP)
    y = jnp.einsum('blgc,gce->blge', p, pool_w).reshape(B, L, POOL_WIDTH)
    return y * pool_scale


def conformer_conv(a, b, dw_w, dw_b, ln_g, ln_b):
    u = a * jax.nn.sigmoid(b)
    y = lax.conv_general_dilated(
        u, dw_w[:, None, :].astype(u.dtype), window_strides=(1,),
        padding=((CONV_PAD, CONV_PAD),), dimension_numbers=('NWC', 'WIO', 'NWC'),
        feature_group_count=CONV_WIDTH)
    y = layer_norm(y + dw_b, ln_g, ln_b)
    return jax.nn.silu(y)


def swiglu(h, w_in, w_out):
    gu = jnp.einsum('bld,de->ble', h, w_in)
    g, u = jnp.split(gu, 2, axis=-1)
    return jnp.einsum('blf,fd->bld', jax.nn.silu(g) * u, w_out)


def setup_inputs(seed: int = 0) -> dict:
    key = jax.random.key(seed)
    ks = jax.random.split(key, 26)
    f32 = jnp.float32
    n = lambda k, s: jax.random.normal(k, s, f32)
    return {
        "x": n(ks[0], (BATCH, SEQ, D_MODEL)),
        "c": n(ks[1], (BATCH, D_MODEL)),
        "ctx": n(ks[2], (BATCH, CTX_LEN, D_MODEL)),
        "c_ctx": n(ks[3], (D_MODEL,)),
        "w_mod": n(ks[4], (DEPTH, D_MODEL, 6 * D_MODEL)) * (0.5 * D_MODEL ** -0.5),
        "b_mod": n(ks[5], (DEPTH, 6 * D_MODEL)) * 0.01,
        "norm1_g": 1.0 + 0.02 * n(ks[6], (DEPTH, D_MODEL)),
        "w_in": n(ks[7], (DEPTH, D_MODEL, IN_WIDTH)) * D_MODEL ** -0.5,
        "q_norm_g": 1.0 + 0.02 * n(ks[8], (DEPTH, ATTN_HEAD_DIM)),
        "k_norm_g": 1.0 + 0.02 * n(ks[9], (DEPTH, ATTN_HEAD_DIM)),
        "lambda_q1": 0.1 * n(ks[10], (DEPTH, ATTN_HEAD_DIM)),
        "lambda_k1": 0.1 * n(ks[11], (DEPTH, ATTN_HEAD_DIM)),
        "lambda_q2": 0.1 * n(ks[12], (DEPTH, ATTN_HEAD_DIM)),
        "lambda_k2": 0.1 * n(ks[13], (DEPTH, ATTN_HEAD_DIM)),
        "subln_g": 1.0 + 0.02 * n(ks[14], (DEPTH, ATTN_V_DIM)),
        "pool_w": n(ks[15], (DEPTH, len(POOL_WINDOWS), POOL_GROUP, POOL_GROUP)) * POOL_GROUP ** -0.5,
        "pool_scale": 1.0 + 0.02 * n(ks[16], (DEPTH, POOL_WIDTH)),
        "conv_dw_w": n(ks[17], (DEPTH, CONV_KERNEL, CONV_WIDTH)) * CONV_KERNEL ** -0.5,
        "conv_dw_b": 0.01 * n(ks[18], (DEPTH, CONV_WIDTH)),
        "conv_ln_g": 1.0 + 0.02 * n(ks[19], (DEPTH, CONV_WIDTH)),
        "conv_ln_b": 0.01 * n(ks[20], (DEPTH, CONV_WIDTH)),
        "w_out": n(ks[21], (DEPTH, MIX_WIDTH, D_MODEL)) * MIX_WIDTH ** -0.5,
        "norm2_g": 1.0 + 0.02 * n(ks[22], (DEPTH, D_MODEL)),
        "w_ffn_in": n(ks[23], (DEPTH, D_MODEL, 2 * FFN_HIDDEN)) * D_MODEL ** -0.5,
        "w_ffn_out": n(ks[24], (DEPTH, FFN_HIDDEN, D_MODEL)) * FFN_HIDDEN ** -0.5,
    }


def reference(x, c, ctx, c_ctx, w_mod, b_mod, norm1_g, w_in, q_norm_g, k_norm_g,
              lambda_q1, lambda_k1, lambda_q2, lambda_k2, subln_g, pool_w, pool_scale,
              conv_dw_w, conv_dw_b, conv_ln_g, conv_ln_b, w_out, norm2_g, w_ffn_in,
              w_ffn_out):
    B, L, _ = x.shape
    rows = L // GRID_W
    cos, sin = axial_rope_tables(rows)
    xc = ctx
    for l in range(DEPTH):
        last = l == DEPTH - 1
        lambda_init = 0.8 - 0.6 * math.exp(-0.3 * l)
        lam = (jnp.exp(jnp.sum(lambda_q1[l] * lambda_k1[l]).astype(jnp.float32))
               - jnp.exp(jnp.sum(lambda_q2[l] * lambda_k2[l]).astype(jnp.float32))
               + lambda_init)
        sh1, sc1, g1, sh2, sc2, g2 = adaln(c, w_mod[l], b_mod[l])
        csh1, csc1, cg1, csh2, csc2, cg2 = adaln(c_ctx, w_mod[l], b_mod[l])

        h = modulate(rms_norm(x, norm1_g[l]), sh1, sc1)
        hc = modulate(rms_norm(xc, norm1_g[l]), csh1, csc1)
        q, k, v, up, ca, cb = project_inputs(h, w_in[l], q_norm_g[l], k_norm_g[l])
        qc, kc, vc, upc, cac, cbc = project_inputs(hc, w_in[l], q_norm_g[l], k_norm_g[l])
        q = apply_rope(q, cos, sin)
        k = apply_rope(k, cos, sin)
        k_all = jnp.concatenate([kc, k], axis=3)
        v_all = jnp.concatenate([vc, v], axis=2)
        attn = diff_attn_post(diff_attention(q, k_all, v_all, lam), subln_g[l], lambda_init)
        pool_o = pool_mixer(up, pool_w[l], pool_scale[l])
        conv_o = conformer_conv(ca, cb, conv_dw_w[l], conv_dw_b[l], conv_ln_g[l], conv_ln_b[l])
        mix = jnp.concatenate([attn, pool_o, conv_o], axis=-1)
        x = x + g1 * jnp.einsum('blm,md->bld', mix, w_out[l])

        x = x + g2 * swiglu(modulate(rms_norm(x, norm2_g[l]), sh2, sc2), w_ffn_in[l], w_ffn_out[l])

        if not last:
            attn_c = diff_attn_post(diff_attention(qc, kc, vc, lam), subln_g[l], lambda_init)
            pool_c = pool_mixer(upc, pool_w[l], pool_scale[l])
            conv_c = conformer_conv(cac, cbc, conv_dw_w[l], conv_dw_b[l], conv_ln_g[l], conv_ln_b[l])
            mix_c = jnp.concatenate([attn_c, pool_c, conv_c], axis=-1)
            xc = xc + cg1 * jnp.einsum('blm,md->bld', mix_c, w_out[l])
            xc = xc + cg2 * swiglu(modulate(rms_norm(xc, norm2_g[l]), csh2, csc2),
                                   w_ffn_in[l], w_ffn_out[l])
    return x
```

```python
import functools
import math

import numpy as np
import jax
import jax.numpy as jnp
from jax import lax
from jax.experimental import pallas as pl
from jax.experimental.pallas import tpu as pltpu

F32 = jnp.float32
BF16 = jnp.bfloat16

GRID_W = 64
HEADS = 4
HEAD_DIM = 64
V_DIM = 2 * HEAD_DIM
QK_W = HEADS * V_DIM
POOL_WINDOWS = (2, 4, 8, 16)
POOL_GROUP = 64
MIX_SIDE_W = 256
CONV_K = 31
CONV_PAD = CONV_K // 2
HALO = 16
ROPE_THETA = 10000.0
EPS = 1e-6

V7X_VMEM_BYTES = 64 * 1024 * 1024
V7X_LANES = 128

MOD_COL_BLOCK = 1536
TOKEN_TILE = 512
Q_TILE = 512


def _vmem_limit(nbytes):
    return int(min(nbytes * 2 + (8 << 20), V7X_VMEM_BYTES - (6 << 20)))


def _mod_kernel(cond_ref, w_ref, b_ref, o_ref):
    c = cond_ref[...]
    a = (c * jax.nn.sigmoid(c)).astype(BF16)
    w = w_ref[0].astype(BF16)
    o_ref[0] = jnp.dot(a, w, preferred_element_type=F32) + b_ref[0]


def _modulation(cond8, w_mod, b_mod):
    depth, d, n = w_mod.shape
    nb = MOD_COL_BLOCK
    return pl.pallas_call(
        _mod_kernel,
        grid=(depth, n // nb),
        in_specs=[
            pl.BlockSpec((8, d), lambda l, j: (0, 0)),
            pl.BlockSpec((1, d, nb), lambda l, j: (l, 0, j)),
            pl.BlockSpec((1, 1, nb), lambda l, j: (l, 0, j)),
        ],
        out_specs=pl.BlockSpec((1, 8, nb), lambda l, j: (l, 0, j)),
        out_shape=jax.ShapeDtypeStruct((depth, 8, n), F32),
        compiler_params=pltpu.CompilerParams(
            dimension_semantics=("arbitrary", "arbitrary"),
            vmem_limit_bytes=_vmem_limit(2 * d * nb * 4 + d * nb * 2)),
        name="adaln_mod",
    )(cond8, w_mod, b_mod.reshape(depth, 1, n))


def _group_rms(z, gmat, g_lane):
    sq = z * z
    hi = sq.astype(BF16)
    lo = (sq - hi.astype(F32)).astype(BF16)
    msq = (jnp.dot(hi, gmat, preferred_element_type=F32)
           + jnp.dot(lo, gmat, preferred_element_type=F32))
    return z * lax.rsqrt(msq + EPS) * g_lane


def _rope(z, cos, sin):
    outs = []
    for h in range(HEADS):
        zh = z[:, V_DIM * h:V_DIM * (h + 1)]
        outs.append(zh * cos + pltpu.roll(zh, V_DIM // 2, axis=1) * sin)
    return jnp.concatenate(outs, axis=1)


def _proj_kernel(*refs, rope):
    if rope:
        (x_ref, mod_ref, g1_ref, w_ref, gmat_ref, qg_ref, kg_ref, cos_ref, sin_ref,
         qT_ref, k_ref, vT_ref, u_ref) = refs
    else:
        (x_ref, mod_ref, g1_ref, w_ref, gmat_ref, qg_ref, kg_ref,
         qT_ref, k_ref, vT_ref, u_ref) = refs
    x = x_ref[0]
    ms = jnp.mean(x * x, axis=-1, keepdims=True)
    h = x * lax.rsqrt(ms + EPS) * g1_ref[...]
    h = h * (1.0 + mod_ref[0, 1:2, :]) + mod_ref[0, 0:1, :]
    proj = jnp.dot(h.astype(BF16), w_ref[...], preferred_element_type=F32)

    gmat = gmat_ref[...]
    q = _group_rms(proj[:, 0:QK_W], gmat, qg_ref[...])
    k = _group_rms(proj[:, QK_W:2 * QK_W], gmat, kg_ref[...])
    if rope:
        cos = cos_ref[...]
        sin = sin_ref[...]
        q = _rope(q, cos, sin)
        k = _rope(k, cos, sin)
    v = proj[:, 2 * QK_W:3 * QK_W]

    qT = q.T
    row = lax.broadcasted_iota(jnp.int32, qT.shape, 0)
    comp0 = (row % HEAD_DIM) < (HEAD_DIM // 2)
    qT_ref[0, 0] = jnp.where(comp0, qT, 0.0).astype(BF16)
    qT_ref[0, 1] = jnp.where(comp0, 0.0, qT).astype(BF16)
    k_ref[0] = k.astype(BF16)
    vT_ref[0, 0] = v.T.astype(BF16)
    u_ref[0] = proj[:, 3 * QK_W:]


def _project(x, modp, g1, w_in, gmat, qg, kg, cos, sin, *, tile):
    b, l, d = x.shape
    e = w_in.shape[1]
    nt = l // tile
    rope = cos is not None
    const = lambda bb, i: (0, 0)
    in_specs = [
        pl.BlockSpec((1, tile, d), lambda bb, i: (bb, i, 0)),
        pl.BlockSpec((1, 8, d), lambda bb, i: (bb, 0, 0)),
        pl.BlockSpec((1, d), const),
        pl.BlockSpec((d, e), const, pipeline_mode=pl.Buffered(1)),
        pl.BlockSpec((QK_W, QK_W), const, pipeline_mode=pl.Buffered(1)),
        pl.BlockSpec((1, QK_W), const),
        pl.BlockSpec((1, QK_W), const),
    ]
    args = [x, modp, g1, w_in, gmat, qg, kg]
    if rope:
        in_specs += [pl.BlockSpec((tile, V_DIM), lambda bb, i: (i, 0)),
                     pl.BlockSpec((tile, V_DIM), lambda bb, i: (i, 0))]
        args += [cos, sin]
    side = e - 3 * QK_W
    out_shape = (
        jax.ShapeDtypeStruct((b, 2, QK_W, l), BF16),
        jax.ShapeDtypeStruct((b, l, QK_W), BF16),
        jax.ShapeDtypeStruct((b, nt, QK_W, tile), BF16),
        jax.ShapeDtypeStruct((b, l, side), F32),
    )
    out_specs = (
        pl.BlockSpec((1, 2, QK_W, tile), lambda bb, i: (bb, 0, 0, i)),
        pl.BlockSpec((1, tile, QK_W), lambda bb, i: (bb, i, 0)),
        pl.BlockSpec((1, 1, QK_W, tile), lambda bb, i: (bb, i, 0, 0)),
        pl.BlockSpec((1, tile, side), lambda bb, i: (bb, i, 0)),
    )
    est = (d * e * 2 + 2 * tile * d * 4 + 3 * tile * e * 4 + 2 * tile * side * 4
           + 8 * tile * QK_W * 4)
    return pl.pallas_call(
        functools.partial(_proj_kernel, rope=rope),
        grid=(b, nt),
        in_specs=in_specs,
        out_specs=out_specs,
        out_shape=out_shape,
        compiler_params=pltpu.CompilerParams(
            dimension_semantics=("arbitrary", "arbitrary"),
            vmem_limit_bytes=_vmem_limit(est)),
        name="proj_rope" if rope else "proj_ctx",
    )(*args)


def _attn_kernel(*refs, n_lat_blocks, kblk, one_minus_init):
    if n_lat_blocks:
        (lam_ref, qT_ref, kc_ref, vTc_ref, k_ref, vT_ref, g_ref, o_ref,
         m_sc, l_sc, acc_sc) = refs
    else:
        lam_ref, qT_ref, kc_ref, vTc_ref, g_ref, o_ref, m_sc, l_sc, acc_sc = refs

    m_sc[...] = jnp.full(m_sc.shape, -jnp.inf, F32)
    l_sc[...] = jnp.zeros(l_sc.shape, F32)
    acc_sc[...] = jnp.zeros(acc_sc.shape, F32)

    def step(kb, vb):
        for c in range(2):
            s = jnp.dot(kb, qT_ref[0, c], preferred_element_type=F32)
            m_old = m_sc[c]
            m_new = jnp.maximum(m_old, jnp.max(s, axis=0, keepdims=True))
            alpha = jnp.exp(m_old - m_new)
            p = jnp.exp(s - m_new)
            l_sc[c] = alpha * l_sc[c] + jnp.sum(p, axis=0, keepdims=True)
            acc_sc[c] = alpha * acc_sc[c] + jnp.dot(vb, p.astype(BF16),
                                                    preferred_element_type=F32)
            m_sc[c] = m_new

    step(kc_ref[0], vTc_ref[0, 0])
    if n_lat_blocks:
        def body(j, carry):
            start = pl.multiple_of(j * kblk, kblk)
            step(k_ref[0, pl.ds(start, kblk), :], vT_ref[0, j])
            return carry
        lax.fori_loop(0, n_lat_blocks, body, 0)

    lam = lam_ref[0]
    o = acc_sc[0] * (1.0 / l_sc[0]) - lam * (acc_sc[1] * (1.0 / l_sc[1]))
    ms = jnp.mean(o * o, axis=0, keepdims=True)
    o = o * lax.rsqrt(ms + EPS) * (g_ref[...] * one_minus_init)
    o_ref[0] = o.T.astype(BF16)


def _attention(lam, qT, kc, vTc, k, vT, g_col, *, qtile, one_minus_init):
    b, _, _, lq = qT.shape
    lc = kc.shape[1]
    nq = lq // qtile
    in_specs = [
        pl.BlockSpec(memory_space=pltpu.SMEM),
        pl.BlockSpec((1, 2, V_DIM, qtile), lambda bb, h, i: (bb, 0, h, i)),
        pl.BlockSpec((1, lc, V_DIM), lambda bb, h, i: (bb, 0, h)),
        pl.BlockSpec((1, 1, V_DIM, lc), lambda bb, h, i: (bb, 0, h, 0)),
    ]
    args = [lam, qT, kc, vTc]
    n_lat, kblk = 0, 0
    est = 4 * lc * V_DIM * 2
    if k is not None:
        lk = k.shape[1]
        n_lat, kblk = vT.shape[1], vT.shape[3]
        in_specs += [
            pl.BlockSpec((1, lk, V_DIM), lambda bb, h, i: (bb, 0, h)),
            pl.BlockSpec((1, n_lat, V_DIM, kblk), lambda bb, h, i: (bb, 0, h, 0)),
        ]
        args += [k, vT]
        est += 4 * lk * V_DIM * 2
    in_specs.append(pl.BlockSpec((V_DIM, 1), lambda bb, h, i: (0, 0)))
    args.append(g_col)
    est += 6 * max(kblk, lc) * qtile * 4 + 4 * V_DIM * qtile * 4
    return pl.pallas_call(
        functools.partial(_attn_kernel, n_lat_blocks=n_lat, kblk=kblk,
                          one_minus_init=one_minus_init),
        grid=(b, HEADS, nq),
        in_specs=in_specs,
        out_specs=pl.BlockSpec((1, qtile, V_DIM), lambda bb, h, i: (bb, i, h)),
        out_shape=jax.ShapeDtypeStruct((b, lq, QK_W), BF16),
        scratch_shapes=[
            pltpu.VMEM((2, 1, qtile), F32),
            pltpu.VMEM((2, 1, qtile), F32),
            pltpu.VMEM((2, V_DIM, qtile), F32),
        ],
        compiler_params=pltpu.CompilerParams(
            dimension_semantics=("arbitrary", "arbitrary", "arbitrary"),
            vmem_limit_bytes=_vmem_limit(est)),
        name="diff_attn" if n_lat else "diff_attn_ctx",
    )(*args)


def _mix_kernel(prev_ref, cur_ref, next_ref, pmask_ref, pw_ref, ps_ref, dw_ref, db_ref,
                lg_ref, lb_ref, o_ref, up_sc, glu_sc, *, tile, seq_len):
    i = pl.program_id(1)
    n = pl.num_programs(1)
    w = MIX_SIDE_W

    def put(rows, blk, valid):
        up = blk[:, 0:w]
        glu = blk[:, w:2 * w] * jax.nn.sigmoid(blk[:, 2 * w:3 * w])
        if valid is not None:
            up = jnp.where(valid, up, 0.0)
            glu = jnp.where(valid, glu, 0.0)
        up_sc[rows, :] = up
        glu_sc[rows, :] = glu

    put(slice(0, HALO), prev_ref[0], i > 0)
    put(slice(HALO, HALO + tile), cur_ref[0], None)
    put(slice(HALO + tile, 2 * HALO + tile), next_ref[0], i < n - 1)

    acc = jnp.zeros((tile, w), F32)
    for kk in range(CONV_K):
        acc = acc + glu_sc[pl.ds(HALO - CONV_PAD + kk, tile), :] * dw_ref[kk:kk + 1, :]
    y = acc + db_ref[...]
    mu = jnp.mean(y, axis=-1, keepdims=True)
    yc = y - mu
    var = jnp.mean(yc * yc, axis=-1, keepdims=True)
    y = yc * lax.rsqrt(var + EPS) * lg_ref[...] + lb_ref[...]
    conv_o = y * jax.nn.sigmoid(y)

    half_max = max(POOL_WINDOWS) // 2
    wsum = jnp.zeros((tile, w), F32)
    for off in range(-half_max, half_max):
        wsum = wsum + (up_sc[pl.ds(HALO + off, tile), :]
                       * pmask_ref[off + half_max:off + half_max + 1, :])
    t = (i * tile + lax.broadcasted_iota(jnp.int32, (tile, w), 0))
    lane = lax.broadcasted_iota(jnp.int32, (tile, w), 1)
    half = jnp.left_shift(1, lane // POOL_GROUP)
    cnt = (jnp.minimum(t + half, seq_len) - jnp.maximum(t - half, 0)).astype(F32)
    p = wsum / cnt - cur_ref[0][:, 0:w]
    pool_o = jnp.dot(p.astype(BF16), pw_ref[...], preferred_element_type=F32) * ps_ref[...]

    o_ref[0] = jnp.concatenate([pool_o, conv_o], axis=1).astype(BF16)


def _mixers(u, pmask, pw, ps, dw, db, lg, lb, *, tile):
    b, l, side = u.shape
    nt = l // tile
    r = tile // HALO
    nh = l // HALO
    w = MIX_SIDE_W
    const = lambda bb, i: (0, 0)
    in_specs = [
        pl.BlockSpec((1, HALO, side), lambda bb, i: (bb, jnp.maximum(i * r - 1, 0), 0)),
        pl.BlockSpec((1, tile, side), lambda bb, i: (bb, i, 0)),
        pl.BlockSpec((1, HALO, side), lambda bb, i: (bb, jnp.minimum((i + 1) * r, nh - 1), 0)),
        pl.BlockSpec(pmask.shape, const),
        pl.BlockSpec((w, w), const),
        pl.BlockSpec((1, w), const),
        pl.BlockSpec((CONV_K, w), const),
        pl.BlockSpec((1, w), const),
        pl.BlockSpec((1, w), const),
        pl.BlockSpec((1, w), const),
    ]
    est = 2 * tile * side * 4 + 4 * (tile + 2 * HALO) * w * 4 + 8 * tile * w * 4
    return pl.pallas_call(
        functools.partial(_mix_kernel, tile=tile, seq_len=l),
        grid=(b, nt),
        in_specs=in_specs,
        out_specs=pl.BlockSpec((1, tile, 2 * w), lambda bb, i: (bb, i, 0)),
        out_shape=jax.ShapeDtypeStruct((b, l, 2 * w), BF16),
        scratch_shapes=[pltpu.VMEM((tile + 2 * HALO, w), F32),
                        pltpu.VMEM((tile + 2 * HALO, w), F32)],
        compiler_params=pltpu.CompilerParams(
            dimension_semantics=("arbitrary", "arbitrary"),
            vmem_limit_bytes=_vmem_limit(est)),
        name="pool_conv",
    )(u, u, u, pmask, pw, ps, dw, db, lg, lb)


def _out_ffn_kernel(x_ref, a_ref, s_ref, mod_ref, wo_ref, n2_ref, wi_ref, wd_ref, o_ref,
                    *, hidden, n_chunks):
    x = x_ref[0]
    mix = jnp.concatenate([a_ref[0], s_ref[0]], axis=1)
    x1 = x + mod_ref[0, 2:3, :] * jnp.dot(mix, wo_ref[...], preferred_element_type=F32)
    ms = jnp.mean(x1 * x1, axis=-1, keepdims=True)
    h = x1 * lax.rsqrt(ms + EPS) * n2_ref[...]
    h = (h * (1.0 + mod_ref[0, 4:5, :]) + mod_ref[0, 3:4, :]).astype(BF16)
    fc = hidden // n_chunks
    acc = jnp.zeros(x.shape, F32)
    for j in range(n_chunks):
        g = jnp.dot(h, wi_ref[:, j * fc:(j + 1) * fc], preferred_element_type=F32)
        u = jnp.dot(h, wi_ref[:, hidden + j * fc:hidden + (j + 1) * fc],
                    preferred_element_type=F32)
        act = (g * jax.nn.sigmoid(g) * u).astype(BF16)
        acc = acc + jnp.dot(act, wd_ref[j * fc:(j + 1) * fc, :], preferred_element_type=F32)
    o_ref[0] = x1 + mod_ref[0, 5:6, :] * acc


def _out_ffn(x, attn, side, modp, w_out, n2, w_ffn_in, w_ffn_out, *, tile):
    b, l, d = x.shape
    hidden = w_ffn_out.shape[0]
    n_chunks = 2
    const = lambda bb, i: (0, 0)
    in_specs = [
        pl.BlockSpec((1, tile, d), lambda bb, i: (bb, i, 0)),
        pl.BlockSpec((1, tile, attn.shape[2]), lambda bb, i: (bb, i, 0)),
        pl.BlockSpec((1, tile, side.shape[2]), lambda bb, i: (bb, i, 0)),
        pl.BlockSpec((1, 8, d), lambda bb, i: (bb, 0, 0)),
        pl.BlockSpec(w_out.shape, const, pipeline_mode=pl.Buffered(1)),
        pl.BlockSpec((1, d), const),
        pl.BlockSpec(w_ffn_in.shape, const, pipeline_mode=pl.Buffered(1)),
        pl.BlockSpec(w_ffn_out.shape, const, pipeline_mode=pl.Buffered(1)),
    ]
    est = ((w_out.size + w_ffn_in.size + w_ffn_out.size) * 2 + 4 * tile * d * 4
           + 2 * tile * d * 2 + 3 * tile * (hidden // n_chunks) * 4 + 3 * tile * d * 4)
    return pl.pallas_call(
        functools.partial(_out_ffn_kernel, hidden=hidden, n_chunks=n_chunks),
        grid=(b, l // tile),
        in_specs=in_specs,
        out_specs=pl.BlockSpec((1, tile, d), lambda bb, i: (bb, i, 0)),
        out_shape=jax.ShapeDtypeStruct((b, l, d), F32),
        compiler_params=pltpu.CompilerParams(
            dimension_semantics=("arbitrary", "arbitrary"),
            vmem_limit_bytes=_vmem_limit(est)),
        name="out_ffn",
    )(x, attn, side, modp, w_out, n2, w_ffn_in, w_ffn_out)


def _head_lane_order():
    comp = np.zeros(V_DIM, np.int32)
    dim = np.zeros(V_DIM, np.int32)
    for j in range(V_DIM):
        second = j // HEAD_DIM
        c = (j % HEAD_DIM) // (HEAD_DIM // 2)
        p = j % (HEAD_DIM // 2)
        comp[j] = c
        dim[j] = 2 * p + second
    return comp, dim


def _qk_column_perm():
    comp, dim = _head_lane_order()
    cols = [h * V_DIM + comp[j] * HEAD_DIM + dim[j] for h in range(HEADS) for j in range(V_DIM)]
    return np.asarray(cols, np.int32)


def _group_matrix():
    comp, _ = _head_lane_order()
    head = np.arange(QK_W) // V_DIM
    c = np.tile(comp, HEADS)
    same = (head[:, None] == head[None, :]) & (c[:, None] == c[None, :])
    return jnp.asarray(same.astype(np.float32) / HEAD_DIM, BF16)


def _rope_lane_tables(seq_len):
    rows = seq_len // GRID_W
    row = jnp.repeat(jnp.arange(rows), GRID_W).astype(F32)
    col = jnp.tile(jnp.arange(GRID_W), rows).astype(F32)
    half = HEAD_DIM // 2
    inv_freq = ROPE_THETA ** (-jnp.arange(0, half, 2, dtype=F32) / half)
    ang = jnp.concatenate([row[:, None] * inv_freq, col[:, None] * inv_freq], axis=-1)
    cos, sin = jnp.cos(ang), jnp.sin(ang)
    cos_l = jnp.tile(cos, (1, 4))
    sin_l = jnp.concatenate([-sin, -sin, sin, sin], axis=1)
    return cos_l, sin_l


def _pool_offset_mask():
    half_max = max(POOL_WINDOWS) // 2
    m = np.zeros((2 * half_max, MIX_SIDE_W), np.float32)
    for gi, wdw in enumerate(POOL_WINDOWS):
        for off in range(-(wdw // 2), wdw - wdw // 2):
            m[off + half_max, gi * POOL_GROUP:(gi + 1) * POOL_GROUP] = 1.0
    return jnp.asarray(m)


def _mod_rows(mod_l, b, d):
    m = mod_l.reshape(8, 6, d)
    pad = jnp.zeros((b, 2, d), F32)
    lat = jnp.concatenate([m[:b], pad], axis=1)
    ctx = jnp.concatenate([jnp.broadcast_to(m[b:b + 1], (b, 6, d)), pad], axis=1)
    return lat, ctx


def kernel(x, c, ctx, c_ctx, w_mod, b_mod, norm1_g, w_in, q_norm_g, k_norm_g, lambda_q1,
           lambda_k1, lambda_q2, lambda_k2, subln_g, pool_w, pool_scale, conv_dw_w, conv_dw_b,
           conv_ln_g, conv_ln_b, w_out, norm2_g, w_ffn_in, w_ffn_out):
    b, l, d = x.shape
    lc = ctx.shape[1]
    depth = w_mod.shape[0]
    assert b + 1 <= 8 and l % TOKEN_TILE == 0 and l % Q_TILE == 0 and lc % HALO == 0

    cond8 = jnp.concatenate([c, c_ctx[None, :], jnp.zeros((8 - b - 1, d), F32)], axis=0)
    mod_all = _modulation(cond8, w_mod, b_mod)

    perm = _qk_column_perm()
    _, lane_dim = _head_lane_order()
    lane_dim = np.tile(lane_dim, HEADS)
    w_q = w_in[:, :, 0:QK_W][:, :, perm]
    w_k = w_in[:, :, QK_W:2 * QK_W][:, :, perm]
    w_in_b = jnp.concatenate([w_q, w_k, w_in[:, :, 2 * QK_W:]], axis=2).astype(BF16)
    w_out_b = w_out.astype(BF16)
    w_ffn_in_b = w_ffn_in.astype(BF16)
    w_ffn_out_b = w_ffn_out.astype(BF16)
    qg_lane = q_norm_g[:, lane_dim] * (HEAD_DIM ** -0.5)
    kg_lane = k_norm_g[:, lane_dim]
    gmat = _group_matrix()
    cos_l, sin_l = _rope_lane_tables(l)
    pmask = _pool_offset_mask()
    pw_bd = jax.vmap(lambda w4: jax.scipy.linalg.block_diag(*[w4[g] for g in range(4)]))(
        pool_w).astype(BF16)

    xc = ctx
    for li in range(depth):
        last = li == depth - 1
        lambda_init = 0.8 - 0.6 * math.exp(-0.3 * li)
        lam = (jnp.exp(jnp.sum(lambda_q1[li] * lambda_k1[li]).astype(F32))
               - jnp.exp(jnp.sum(lambda_q2[li] * lambda_k2[li]).astype(F32))
               + lambda_init).reshape(1)
        mod_lat, mod_ctx = _mod_rows(mod_all[li], b, d)
        g1 = norm1_g[li][None, :]
        n2 = norm2_g[li][None, :]
        qg = qg_lane[li][None, :]
        kg = kg_lane[li][None, :]
        sg = subln_g[li][:, None]
        mix_params = (pmask, pw_bd[li], pool_scale[li][None, :], conv_dw_w[li],
                      conv_dw_b[li][None, :], conv_ln_g[li][None, :], conv_ln_b[li][None, :])

        qT, k, vT, u = _project(x, mod_lat, g1, w_in_b[li], gmat, qg, kg, cos_l, sin_l,
                                tile=TOKEN_TILE)
        qTc, kc, vTc, uc = _project(xc, mod_ctx, g1, w_in_b[li], gmat, qg, kg, None, None,
                                    tile=lc)
        attn = _attention(lam, qT, kc, vTc, k, vT, sg, qtile=Q_TILE,
                          one_minus_init=1.0 - lambda_init)
        side = _mixers(u, *mix_params, tile=TOKEN_TILE)
        x = _out_ffn(x, attn, side, mod_lat, w_out_b[li], n2, w_ffn_in_b[li], w_ffn_out_b[li],
                     tile=TOKEN_TILE)
        if not last:
            attn_c = _attention(lam, qTc, kc, vTc, None, None, sg, qtile=lc,
                                one_minus_init=1.0 - lambda_init)
            side_c = _mixers(uc, *mix_params, tile=lc)
            xc = _out_ffn(xc, attn_c, side_c, mod_ctx, w_out_b[li], n2, w_ffn_in_b[li],
                          w_ffn_out_b[li], tile=lc)
    return x
```

```python
import functools
import math

import numpy as np
import jax
import jax.numpy as jnp
from jax import lax
from jax.experimental import pallas as pl
from jax.experimental.pallas import tpu as pltpu

F32 = jnp.float32
BF16 = jnp.bfloat16

GRID_W = 64
HEADS = 4
HEAD_DIM = 64
V_DIM = 2 * HEAD_DIM
QK_W = HEADS * V_DIM
POOL_WINDOWS = (2, 4, 8, 16)
POOL_GROUP = 64
MIX_SIDE_W = 256
CONV_K = 31
CONV_PAD = CONV_K // 2
HALO = 16
ROPE_THETA = 10000.0
EPS = 1e-6

V7X_VMEM_BYTES = 64 * 1024 * 1024
V7X_LANES = 128

MOD_COL_BLOCK = 1536
TOKEN_TILE = 512
Q_TILE = 512
SCORE_BOUND_LOG2 = 60.0
SCORE_BOUND_MARGIN = 1.05


def _vmem_limit(nbytes):
    return int(min(nbytes * 2 + (8 << 20), V7X_VMEM_BYTES - (6 << 20)))


def _mod_kernel(cond_ref, w_ref, b_ref, o_ref):
    c = cond_ref[...]
    a = (c * jax.nn.sigmoid(c)).astype(BF16)
    w = w_ref[0].astype(BF16)
    o_ref[0] = jnp.dot(a, w, preferred_element_type=F32) + b_ref[0]


def _modulation(cond8, w_mod, b_mod):
    depth, d, n = w_mod.shape
    nb = MOD_COL_BLOCK
    return pl.pallas_call(
        _mod_kernel,
        grid=(depth, n // nb),
        in_specs=[
            pl.BlockSpec((8, d), lambda l, j: (0, 0)),
            pl.BlockSpec((1, d, nb), lambda l, j: (l, 0, j)),
            pl.BlockSpec((1, 1, nb), lambda l, j: (l, 0, j)),
        ],
        out_specs=pl.BlockSpec((1, 8, nb), lambda l, j: (l, 0, j)),
        out_shape=jax.ShapeDtypeStruct((depth, 8, n), F32),
        compiler_params=pltpu.CompilerParams(
            dimension_semantics=("arbitrary", "arbitrary"),
            vmem_limit_bytes=_vmem_limit(2 * d * nb * 4 + d * nb * 2)),
        name="adaln_mod",
    )(cond8, w_mod, b_mod.reshape(depth, 1, n))


def _group_rms(z, gmat, g_lane):
    sq = z * z
    hi = sq.astype(BF16)
    lo = (sq - hi.astype(F32)).astype(BF16)
    msq = (jnp.dot(hi, gmat, preferred_element_type=F32)
           + jnp.dot(lo, gmat, preferred_element_type=F32))
    return z * lax.rsqrt(msq + EPS) * g_lane


def _rope(z, cos, sin):
    outs = []
    for h in range(HEADS):
        zh = z[:, V_DIM * h:V_DIM * (h + 1)]
        outs.append(zh * cos + pltpu.roll(zh, V_DIM // 2, axis=1) * sin)
    return jnp.concatenate(outs, axis=1)


def _proj_kernel(*refs, rope):
    if rope:
        (x_ref, mod_ref, g1_ref, w_ref, gmat_ref, qg_ref, kg_ref, cos_ref, sin_ref,
         qT_ref, k_ref, vT_ref, u_ref) = refs
    else:
        (x_ref, mod_ref, g1_ref, w_ref, gmat_ref, qg_ref, kg_ref,
         qT_ref, k_ref, vT_ref, u_ref) = refs
    x = x_ref[0]
    ms = jnp.mean(x * x, axis=-1, keepdims=True)
    h = x * lax.rsqrt(ms + EPS) * g1_ref[...]
    h = h * (1.0 + mod_ref[0, 1:2, :]) + mod_ref[0, 0:1, :]
    proj = jnp.dot(h.astype(BF16), w_ref[...], preferred_element_type=F32)

    gmat = gmat_ref[...]
    q = _group_rms(proj[:, 0:QK_W], gmat, qg_ref[...])
    k = _group_rms(proj[:, QK_W:2 * QK_W], gmat, kg_ref[...])
    if rope:
        cos = cos_ref[...]
        sin = sin_ref[...]
        q = _rope(q, cos, sin)
        k = _rope(k, cos, sin)
    v = proj[:, 2 * QK_W:3 * QK_W]

    qT = q.T
    row = lax.broadcasted_iota(jnp.int32, qT.shape, 0)
    comp0 = (row % HEAD_DIM) < (HEAD_DIM // 2)
    qT_ref[0, 0] = jnp.where(comp0, qT, 0.0).astype(BF16)
    qT_ref[0, 1] = jnp.where(comp0, 0.0, qT).astype(BF16)
    k_ref[0] = k.astype(BF16)
    vT_ref[0, 0] = v.T.astype(BF16)
    u_ref[0] = proj[:, 3 * QK_W:]


def _project(x, modp, g1, w_in, gmat, qg, kg, cos, sin, *, tile):
    b, l, d = x.shape
    e = w_in.shape[1]
    nt = l // tile
    rope = cos is not None
    const = lambda bb, i: (0, 0)
    in_specs = [
        pl.BlockSpec((1, tile, d), lambda bb, i: (bb, i, 0)),
        pl.BlockSpec((1, 8, d), lambda bb, i: (bb, 0, 0)),
        pl.BlockSpec((1, d), const),
        pl.BlockSpec((d, e), const, pipeline_mode=pl.Buffered(1)),
        pl.BlockSpec((QK_W, QK_W), const, pipeline_mode=pl.Buffered(1)),
        pl.BlockSpec((1, QK_W), const),
        pl.BlockSpec((1, QK_W), const),
    ]
    args = [x, modp, g1, w_in, gmat, qg, kg]
    if rope:
        in_specs += [pl.BlockSpec((tile, V_DIM), lambda bb, i: (i, 0)),
                     pl.BlockSpec((tile, V_DIM), lambda bb, i: (i, 0))]
        args += [cos, sin]
    side = e - 3 * QK_W
    out_shape = (
        jax.ShapeDtypeStruct((b, 2, QK_W, l), BF16),
        jax.ShapeDtypeStruct((b, l, QK_W), BF16),
        jax.ShapeDtypeStruct((b, nt, QK_W, tile), BF16),
        jax.ShapeDtypeStruct((b, l, side), F32),
    )
    out_specs = (
        pl.BlockSpec((1, 2, QK_W, tile), lambda bb, i: (bb, 0, 0, i)),
        pl.BlockSpec((1, tile, QK_W), lambda bb, i: (bb, i, 0)),
        pl.BlockSpec((1, 1, QK_W, tile), lambda bb, i: (bb, i, 0, 0)),
        pl.BlockSpec((1, tile, side), lambda bb, i: (bb, i, 0)),
    )
    est = (d * e * 2 + 2 * tile * d * 4 + 3 * tile * e * 4 + 2 * tile * side * 4
           + 8 * tile * QK_W * 4)
    return pl.pallas_call(
        functools.partial(_proj_kernel, rope=rope),
        grid=(b, nt),
        in_specs=in_specs,
        out_specs=out_specs,
        out_shape=out_shape,
        compiler_params=pltpu.CompilerParams(
            dimension_semantics=("arbitrary", "arbitrary"),
            vmem_limit_bytes=_vmem_limit(est)),
        name="proj_rope" if rope else "proj_ctx",
    )(*args)


def _attn_kernel(*refs, n_lat_blocks, kblk, one_minus_init):
    if n_lat_blocks:
        (lam_ref, flag_ref, qT_ref, kc_ref, vTc_ref, k_ref, vT_ref, g_ref, o_ref,
         m_sc, l_sc, acc_sc, s_sc, mb_sc, p_sc, al_sc) = refs
    else:
        lam_ref, flag_ref, qT_ref, kc_ref, vTc_ref, g_ref, o_ref, m_sc, l_sc, acc_sc = refs

    def context_block(stabilise):
        kb = kc_ref[0]
        vb = vTc_ref[0, 0]
        for c in range(2):
            s = jnp.dot(kb, qT_ref[0, c], preferred_element_type=F32)
            if stabilise:
                m = jnp.max(s, axis=0, keepdims=True)
                m_sc[c] = m
                s = s - m
            p = jnp.exp2(s)
            l_sc[c] = jnp.sum(p, axis=0, keepdims=True)
            acc_sc[c] = jnp.dot(vb, p.astype(BF16), preferred_element_type=F32)

    def bounded_path():
        context_block(False)
        if not n_lat_blocks:
            return

        def probs(j, slot):
            start = pl.multiple_of(j * kblk, kblk)
            kj = k_ref[0, pl.ds(start, kblk), :]
            for c in range(2):
                p = jnp.exp2(jnp.dot(kj, qT_ref[0, c], preferred_element_type=F32))
                l_sc[c] = l_sc[c] + jnp.sum(p, axis=0, keepdims=True)
                p_sc[slot, c] = p.astype(BF16)

        def values(j, slot):
            vj = vT_ref[0, j]
            for c in range(2):
                acc_sc[c] = acc_sc[c] + jnp.dot(vj, p_sc[slot, c],
                                                preferred_element_type=F32)

        n = n_lat_blocks
        probs(0, 0)

        def body(jj, carry):
            j = 2 * jj + 1
            probs(j, 1)
            values(j - 1, 0)
            probs(j + 1, 0)
            values(j, 1)
            return carry
        lax.fori_loop(0, (n - 2) // 2, body, 0)
        probs(n - 1, 1)
        values(n - 2, 0)
        values(n - 1, 1)

    def online_path():
        context_block(True)
        if not n_lat_blocks:
            return

        def scores(j, slot):
            start = pl.multiple_of(j * kblk, kblk)
            kj = k_ref[0, pl.ds(start, kblk), :]
            for c in range(2):
                s = jnp.dot(kj, qT_ref[0, c], preferred_element_type=F32)
                s_sc[slot, c] = s
                mb_sc[slot, c] = jnp.max(s, axis=0, keepdims=True)

        def softmax(slot):
            for c in range(2):
                m_old = m_sc[c]
                m_new = jnp.maximum(m_old, mb_sc[slot, c])
                alpha = jnp.exp2(m_old - m_new)
                p = jnp.exp2(s_sc[slot, c] - m_new)
                l_sc[c] = alpha * l_sc[c] + jnp.sum(p, axis=0, keepdims=True)
                p_sc[slot, c] = p.astype(BF16)
                al_sc[slot, c] = alpha
                m_sc[c] = m_new

        def values(j, slot):
            vj = vT_ref[0, j]
            for c in range(2):
                acc_sc[c] = (al_sc[slot, c] * acc_sc[c]
                             + jnp.dot(vj, p_sc[slot, c], preferred_element_type=F32))

        n = n_lat_blocks
        scores(0, 0)
        scores(1, 1)
        softmax(0)

        def body(jj, carry):
            j = 2 * jj + 1
            scores(j + 1, 0)
            values(j - 1, 0)
            softmax(1)
            scores(j + 2, 1)
            values(j, 1)
            softmax(0)
            return carry
        lax.fori_loop(0, (n - 2) // 2, body, 0)
        values(n - 2, 0)
        softmax(1)
        values(n - 1, 1)

    lax.cond(flag_ref[0] != 0, bounded_path, online_path)

    lam = lam_ref[0]
    o = acc_sc[0] * (1.0 / l_sc[0]) - lam * (acc_sc[1] * (1.0 / l_sc[1]))
    ms = jnp.mean(o * o, axis=0, keepdims=True)
    o = o * lax.rsqrt(ms + EPS) * (g_ref[...] * one_minus_init)
    o_ref[0] = o.T.astype(BF16)


def _attention(lam, bounded, qT, kc, vTc, k, vT, g_col, *, qtile, one_minus_init):
    b, _, _, lq = qT.shape
    lc = kc.shape[1]
    nq = lq // qtile
    in_specs = [
        pl.BlockSpec(memory_space=pltpu.SMEM),
        pl.BlockSpec(memory_space=pltpu.SMEM),
        pl.BlockSpec((1, 2, V_DIM, qtile), lambda bb, h, i: (bb, 0, h, i)),
        pl.BlockSpec((1, lc, V_DIM), lambda bb, h, i: (bb, 0, h)),
        pl.BlockSpec((1, 1, V_DIM, lc), lambda bb, h, i: (bb, 0, h, 0)),
    ]
    args = [lam, bounded, qT, kc, vTc]
    n_lat, kblk = 0, 0
    est = 4 * lc * V_DIM * 2
    if k is not None:
        lk = k.shape[1]
        n_lat, kblk = vT.shape[1], vT.shape[3]
        in_specs += [
            pl.BlockSpec((1, lk, V_DIM), lambda bb, h, i: (bb, 0, h)),
            pl.BlockSpec((1, n_lat, V_DIM, kblk), lambda bb, h, i: (bb, 0, h, 0)),
        ]
        args += [k, vT]
        est += 4 * lk * V_DIM * 2
    in_specs.append(pl.BlockSpec((V_DIM, 1), lambda bb, h, i: (0, 0)))
    args.append(g_col)
    est += 6 * max(kblk, lc) * qtile * 4 + 4 * V_DIM * qtile * 4
    scratch = [
        pltpu.VMEM((2, 1, qtile), F32),
        pltpu.VMEM((2, 1, qtile), F32),
        pltpu.VMEM((2, V_DIM, qtile), F32),
    ]
    if n_lat:
        assert n_lat >= 2 and n_lat % 2 == 0
        scratch += [
            pltpu.VMEM((2, 2, kblk, qtile), F32),
            pltpu.VMEM((2, 2, 1, qtile), F32),
            pltpu.VMEM((2, 2, kblk, qtile), BF16),
            pltpu.VMEM((2, 2, 1, qtile), F32),
        ]
    return pl.pallas_call(
        functools.partial(_attn_kernel, n_lat_blocks=n_lat, kblk=kblk,
                          one_minus_init=one_minus_init),
        grid=(b, HEADS, nq),
        in_specs=in_specs,
        out_specs=pl.BlockSpec((1, qtile, V_DIM), lambda bb, h, i: (bb, i, h)),
        out_shape=jax.ShapeDtypeStruct((b, lq, QK_W), BF16),
        scratch_shapes=scratch,
        compiler_params=pltpu.CompilerParams(
            dimension_semantics=("arbitrary", "arbitrary", "arbitrary"),
            vmem_limit_bytes=_vmem_limit(est)),
        name="diff_attn" if n_lat else "diff_attn_ctx",
    )(*args)


def _mix_kernel(prev_ref, cur_ref, next_ref, pmask_ref, pw_ref, ps_ref, dw_ref, db_ref,
                lg_ref, lb_ref, o_ref, up_sc, glu_sc, *, tile, seq_len):
    i = pl.program_id(1)
    n = pl.num_programs(1)
    w = MIX_SIDE_W

    def put(rows, blk, valid):
        up = blk[:, 0:w]
        glu = blk[:, w:2 * w] * jax.nn.sigmoid(blk[:, 2 * w:3 * w])
        if valid is not None:
            up = jnp.where(valid, up, 0.0)
            glu = jnp.where(valid, glu, 0.0)
        up_sc[rows, :] = up
        glu_sc[rows, :] = glu

    put(slice(0, HALO), prev_ref[0], i > 0)
    put(slice(HALO, HALO + tile), cur_ref[0], None)
    put(slice(HALO + tile, 2 * HALO + tile), next_ref[0], i < n - 1)

    acc = jnp.zeros((tile, w), F32)
    for kk in range(CONV_K):
        acc = acc + glu_sc[pl.ds(HALO - CONV_PAD + kk, tile), :] * dw_ref[kk:kk + 1, :]
    y = acc + db_ref[...]
    mu = jnp.mean(y, axis=-1, keepdims=True)
    yc = y - mu
    var = jnp.mean(yc * yc, axis=-1, keepdims=True)
    y = yc * lax.rsqrt(var + EPS) * lg_ref[...] + lb_ref[...]
    conv_o = y * jax.nn.sigmoid(y)

    half_max = max(POOL_WINDOWS) // 2
    wsum = jnp.zeros((tile, w), F32)
    for off in range(-half_max, half_max):
        wsum = wsum + (up_sc[pl.ds(HALO + off, tile), :]
                       * pmask_ref[off + half_max:off + half_max + 1, :])
    t = (i * tile + lax.broadcasted_iota(jnp.int32, (tile, w), 0))
    lane = lax.broadcasted_iota(jnp.int32, (tile, w), 1)
    half = jnp.left_shift(1, lane // POOL_GROUP)
    cnt = (jnp.minimum(t + half, seq_len) - jnp.maximum(t - half, 0)).astype(F32)
    p = wsum / cnt - cur_ref[0][:, 0:w]
    pool_o = jnp.dot(p.astype(BF16), pw_ref[...], preferred_element_type=F32) * ps_ref[...]

    o_ref[0] = jnp.concatenate([pool_o, conv_o], axis=1).astype(BF16)


def _mixers(u, pmask, pw, ps, dw, db, lg, lb, *, tile):
    b, l, side = u.shape
    nt = l // tile
    r = tile // HALO
    nh = l // HALO
    w = MIX_SIDE_W
    const = lambda bb, i: (0, 0)
    in_specs = [
        pl.BlockSpec((1, HALO, side), lambda bb, i: (bb, jnp.maximum(i * r - 1, 0), 0)),
        pl.BlockSpec((1, tile, side), lambda bb, i: (bb, i, 0)),
        pl.BlockSpec((1, HALO, side), lambda bb, i: (bb, jnp.minimum((i + 1) * r, nh - 1), 0)),
        pl.BlockSpec(pmask.shape, const),
        pl.BlockSpec((w, w), const),
        pl.BlockSpec((1, w), const),
        pl.BlockSpec((CONV_K, w), const),
        pl.BlockSpec((1, w), const),
        pl.BlockSpec((1, w), const),
        pl.BlockSpec((1, w), const),
    ]
    est = 2 * tile * side * 4 + 4 * (tile + 2 * HALO) * w * 4 + 8 * tile * w * 4
    return pl.pallas_call(
        functools.partial(_mix_kernel, tile=tile, seq_len=l),
        grid=(b, nt),
        in_specs=in_specs,
        out_specs=pl.BlockSpec((1, tile, 2 * w), lambda bb, i: (bb, i, 0)),
        out_shape=jax.ShapeDtypeStruct((b, l, 2 * w), BF16),
        scratch_shapes=[pltpu.VMEM((tile + 2 * HALO, w), F32),
                        pltpu.VMEM((tile + 2 * HALO, w), F32)],
        compiler_params=pltpu.CompilerParams(
            dimension_semantics=("arbitrary", "arbitrary"),
            vmem_limit_bytes=_vmem_limit(est)),
        name="pool_conv",
    )(u, u, u, pmask, pw, ps, dw, db, lg, lb)


def _out_ffn_kernel(x_ref, a_ref, s_ref, mod_ref, wo_ref, n2_ref, wi_ref, wd_ref, o_ref,
                    *, hidden, n_chunks):
    x = x_ref[0]
    mix = jnp.concatenate([a_ref[0], s_ref[0]], axis=1)
    x1 = x + mod_ref[0, 2:3, :] * jnp.dot(mix, wo_ref[...], preferred_element_type=F32)
    ms = jnp.mean(x1 * x1, axis=-1, keepdims=True)
    h = x1 * lax.rsqrt(ms + EPS) * n2_ref[...]
    h = (h * (1.0 + mod_ref[0, 4:5, :]) + mod_ref[0, 3:4, :]).astype(BF16)
    fc = hidden // n_chunks
    acc = jnp.zeros(x.shape, F32)
    for j in range(n_chunks):
        g = jnp.dot(h, wi_ref[:, j * fc:(j + 1) * fc], preferred_element_type=F32)
        u = jnp.dot(h, wi_ref[:, hidden + j * fc:hidden + (j + 1) * fc],
                    preferred_element_type=F32)
        act = (g * jax.nn.sigmoid(g) * u).astype(BF16)
        acc = acc + jnp.dot(act, wd_ref[j * fc:(j + 1) * fc, :], preferred_element_type=F32)
    o_ref[0] = x1 + mod_ref[0, 5:6, :] * acc


def _out_ffn(x, attn, side, modp, w_out, n2, w_ffn_in, w_ffn_out, *, tile):
    b, l, d = x.shape
    hidden = w_ffn_out.shape[0]
    n_chunks = 2
    const = lambda bb, i: (0, 0)
    in_specs = [
        pl.BlockSpec((1, tile, d), lambda bb, i: (bb, i, 0)),
        pl.BlockSpec((1, tile, attn.shape[2]), lambda bb, i: (bb, i, 0)),
        pl.BlockSpec((1, tile, side.shape[2]), lambda bb, i: (bb, i, 0)),
        pl.BlockSpec((1, 8, d), lambda bb, i: (bb, 0, 0)),
        pl.BlockSpec(w_out.shape, const, pipeline_mode=pl.Buffered(1)),
        pl.BlockSpec((1, d), const),
        pl.BlockSpec(w_ffn_in.shape, const, pipeline_mode=pl.Buffered(1)),
        pl.BlockSpec(w_ffn_out.shape, const, pipeline_mode=pl.Buffered(1)),
    ]
    est = ((w_out.size + w_ffn_in.size + w_ffn_out.size) * 2 + 4 * tile * d * 4
           + 2 * tile * d * 2 + 3 * tile * (hidden // n_chunks) * 4 + 3 * tile * d * 4)
    return pl.pallas_call(
        functools.partial(_out_ffn_kernel, hidden=hidden, n_chunks=n_chunks),
        grid=(b, l // tile),
        in_specs=in_specs,
        out_specs=pl.BlockSpec((1, tile, d), lambda bb, i: (bb, i, 0)),
        out_shape=jax.ShapeDtypeStruct((b, l, d), F32),
        compiler_params=pltpu.CompilerParams(
            dimension_semantics=("arbitrary", "arbitrary"),
            vmem_limit_bytes=_vmem_limit(est)),
        name="out_ffn",
    )(x, attn, side, modp, w_out, n2, w_ffn_in, w_ffn_out)


def _head_lane_order():
    comp = np.zeros(V_DIM, np.int32)
    dim = np.zeros(V_DIM, np.int32)
    for j in range(V_DIM):
        second = j // HEAD_DIM
        c = (j % HEAD_DIM) // (HEAD_DIM // 2)
        p = j % (HEAD_DIM // 2)
        comp[j] = c
        dim[j] = 2 * p + second
    return comp, dim


def _qk_column_perm():
    comp, dim = _head_lane_order()
    cols = [h * V_DIM + comp[j] * HEAD_DIM + dim[j] for h in range(HEADS) for j in range(V_DIM)]
    return np.asarray(cols, np.int32)


def _group_matrix():
    comp, _ = _head_lane_order()
    head = np.arange(QK_W) // V_DIM
    c = np.tile(comp, HEADS)
    same = (head[:, None] == head[None, :]) & (c[:, None] == c[None, :])
    return jnp.asarray(same.astype(np.float32) / HEAD_DIM, BF16)


def _rope_lane_tables(seq_len):
    rows = seq_len // GRID_W
    row = jnp.repeat(jnp.arange(rows), GRID_W).astype(F32)
    col = jnp.tile(jnp.arange(GRID_W), rows).astype(F32)
    half = HEAD_DIM // 2
    inv_freq = ROPE_THETA ** (-jnp.arange(0, half, 2, dtype=F32) / half)
    ang = jnp.concatenate([row[:, None] * inv_freq, col[:, None] * inv_freq], axis=-1)
    cos, sin = jnp.cos(ang), jnp.sin(ang)
    cos_l = jnp.tile(cos, (1, 4))
    sin_l = jnp.concatenate([-sin, -sin, sin, sin], axis=1)
    return cos_l, sin_l


def _pool_offset_mask():
    half_max = max(POOL_WINDOWS) // 2
    m = np.zeros((2 * half_max, MIX_SIDE_W), np.float32)
    for gi, wdw in enumerate(POOL_WINDOWS):
        for off in range(-(wdw // 2), wdw - wdw // 2):
            m[off + half_max, gi * POOL_GROUP:(gi + 1) * POOL_GROUP] = 1.0
    return jnp.asarray(m)


def _mod_rows(mod_l, b, d):
    m = mod_l.reshape(8, 6, d)
    pad = jnp.zeros((b, 2, d), F32)
    lat = jnp.concatenate([m[:b], pad], axis=1)
    ctx = jnp.concatenate([jnp.broadcast_to(m[b:b + 1], (b, 6, d)), pad], axis=1)
    return lat, ctx


def kernel(x, c, ctx, c_ctx, w_mod, b_mod, norm1_g, w_in, q_norm_g, k_norm_g, lambda_q1,
           lambda_k1, lambda_q2, lambda_k2, subln_g, pool_w, pool_scale, conv_dw_w, conv_dw_b,
           conv_ln_g, conv_ln_b, w_out, norm2_g, w_ffn_in, w_ffn_out):
    b, l, d = x.shape
    lc = ctx.shape[1]
    depth = w_mod.shape[0]
    assert b + 1 <= 8 and l % TOKEN_TILE == 0 and l % Q_TILE == 0 and lc % HALO == 0

    cond8 = jnp.concatenate([c, c_ctx[None, :], jnp.zeros((8 - b - 1, d), F32)], axis=0)
    mod_all = _modulation(cond8, w_mod, b_mod)

    perm = _qk_column_perm()
    _, lane_dim = _head_lane_order()
    lane_dim = np.tile(lane_dim, HEADS)
    w_q = w_in[:, :, 0:QK_W][:, :, perm]
    w_k = w_in[:, :, QK_W:2 * QK_W][:, :, perm]
    w_in_b = jnp.concatenate([w_q, w_k, w_in[:, :, 2 * QK_W:]], axis=2).astype(BF16)
    w_out_b = w_out.astype(BF16)
    w_ffn_in_b = w_ffn_in.astype(BF16)
    w_ffn_out_b = w_ffn_out.astype(BF16)
    qg_lane = q_norm_g[:, lane_dim] * (HEAD_DIM ** -0.5 * math.log2(math.e))
    kg_lane = k_norm_g[:, lane_dim]
    gmat = _group_matrix()
    cos_l, sin_l = _rope_lane_tables(l)
    pmask = _pool_offset_mask()
    pw_bd = jax.vmap(lambda w4: jax.scipy.linalg.block_diag(*[w4[g] for g in range(4)]))(
        pool_w).astype(BF16)

    xc = ctx
    for li in range(depth):
        last = li == depth - 1
        lambda_init = 0.8 - 0.6 * math.exp(-0.3 * li)
        lam = (jnp.exp(jnp.sum(lambda_q1[li] * lambda_k1[li]).astype(F32))
               - jnp.exp(jnp.sum(lambda_q2[li] * lambda_k2[li]).astype(F32))
               + lambda_init).reshape(1)
        mod_lat, mod_ctx = _mod_rows(mod_all[li], b, d)
        g1 = norm1_g[li][None, :]
        n2 = norm2_g[li][None, :]
        qg = qg_lane[li][None, :]
        kg = kg_lane[li][None, :]
        sg = subln_g[li][:, None]
        score_bound = (HEAD_DIM * SCORE_BOUND_MARGIN * jnp.max(jnp.abs(qg_lane[li]))
                       * jnp.max(jnp.abs(kg_lane[li])))
        bounded = (score_bound <= SCORE_BOUND_LOG2).astype(jnp.int32).reshape(1)
        mix_params = (pmask, pw_bd[li], pool_scale[li][None, :], conv_dw_w[li],
                      conv_dw_b[li][None, :], conv_ln_g[li][None, :], conv_ln_b[li][None, :])

        qT, k, vT, u = _project(x, mod_lat, g1, w_in_b[li], gmat, qg, kg, cos_l, sin_l,
                                tile=TOKEN_TILE)
        qTc, kc, vTc, uc = _project(xc, mod_ctx, g1, w_in_b[li], gmat, qg, kg, None, None,
                                    tile=lc)
        attn = _attention(lam, bounded, qT, kc, vTc, k, vT, sg, qtile=Q_TILE,
                          one_minus_init=1.0 - lambda_init)
        side = _mixers(u, *mix_params, tile=TOKEN_TILE)
        x = _out_ffn(x, attn, side, mod_lat, w_out_b[li], n2, w_ffn_in_b[li], w_ffn_out_b[li],
                     tile=TOKEN_TILE)
        if not last:
            attn_c = _attention(lam, bounded, qTc, kc, vTc, None, None, sg, qtile=lc,
                                one_minus_init=1.0 - lambda_init)
            side_c = _mixers(uc, *mix_params, tile=lc)
            xc = _out_ffn(xc, attn_c, side_c, mod_ctx, w_out_b[li], n2, w_ffn_in_b[li],
                          w_ffn_out_b[li], tile=lc)
    return x
```

```python
import functools
import math

import numpy as np
import jax
import jax.numpy as jnp
from jax import lax
from jax.experimental import pallas as pl
from jax.experimental.pallas import tpu as pltpu

F32 = jnp.float32
BF16 = jnp.bfloat16

GRID_W = 64
HEADS = 4
HEAD_DIM = 64
V_DIM = 2 * HEAD_DIM
QK_W = HEADS * V_DIM
POOL_WINDOWS = (2, 4, 8, 16)
POOL_GROUP = 64
MIX_SIDE_W = 256
CONV_K = 31
CONV_PAD = CONV_K // 2
HALO = 16
ROPE_THETA = 10000.0
EPS = 1e-6

V7X_VMEM_BYTES = 64 * 1024 * 1024
V7X_LANES = 128
V7X_MXU_WIDTH = 256
SUBLANES = 8

CONV_ROW_CHUNK = 64
CONV_SPAN_PAD = (HALO - CONV_PAD + CONV_K - 1) // SUBLANES * SUBLANES

MOD_COL_BLOCK = 1536
TOKEN_TILE = 512
Q_TILE = 1024
SCORE_BOUND_LOG2 = 60.0
SCORE_BOUND_MARGIN = 1.05


def _vmem_limit(nbytes):
    return int(min(nbytes * 2 + (8 << 20), V7X_VMEM_BYTES - (6 << 20)))


def _mod_kernel(cond_ref, w_ref, b_ref, o_ref):
    c = cond_ref[...]
    a = (c * jax.nn.sigmoid(c)).astype(BF16)
    w = w_ref[0].astype(BF16)
    o_ref[0] = jnp.dot(a, w, preferred_element_type=F32) + b_ref[0]


def _modulation(cond8, w_mod, b_mod):
    depth, d, n = w_mod.shape
    nb = MOD_COL_BLOCK
    return pl.pallas_call(
        _mod_kernel,
        grid=(depth, n // nb),
        in_specs=[
            pl.BlockSpec((8, d), lambda l, j: (0, 0)),
            pl.BlockSpec((1, d, nb), lambda l, j: (l, 0, j)),
            pl.BlockSpec((1, 1, nb), lambda l, j: (l, 0, j)),
        ],
        out_specs=pl.BlockSpec((1, 8, nb), lambda l, j: (l, 0, j)),
        out_shape=jax.ShapeDtypeStruct((depth, 8, n), F32),
        compiler_params=pltpu.CompilerParams(
            dimension_semantics=("arbitrary", "arbitrary"),
            vmem_limit_bytes=_vmem_limit(2 * d * nb * 4 + d * nb * 2)),
        name="adaln_mod",
    )(cond8, w_mod, b_mod.reshape(depth, 1, n))


def _group_rms(z, gmat, g_lane):
    msq = jnp.dot((z * z).astype(BF16), gmat, preferred_element_type=F32)
    return z * lax.rsqrt(msq + EPS) * g_lane


def _rope(z, cos, sin):
    outs = []
    for h in range(HEADS):
        zh = z[:, V_DIM * h:V_DIM * (h + 1)]
        outs.append(zh * cos + pltpu.roll(zh, V_DIM // 2, axis=1) * sin)
    return jnp.concatenate(outs, axis=1)


def _proj_kernel(*refs, rope):
    if rope:
        (x_ref, mod_ref, g1_ref, w_ref, gmat_ref, qg_ref, kg_ref, cos_ref, sin_ref,
         qT_ref, k_ref, vT_ref, u_ref) = refs
    else:
        (x_ref, mod_ref, g1_ref, w_ref, gmat_ref, qg_ref, kg_ref,
         qT_ref, k_ref, vT_ref, u_ref) = refs
    x = x_ref[0]
    ms = jnp.mean(x * x, axis=-1, keepdims=True)
    h = x * lax.rsqrt(ms + EPS) * g1_ref[...]
    h = h * (1.0 + mod_ref[0, 1:2, :]) + mod_ref[0, 0:1, :]
    proj = jnp.dot(h.astype(BF16), w_ref[...], preferred_element_type=F32)

    gmat = gmat_ref[...]
    q = _group_rms(proj[:, 0:QK_W], gmat, qg_ref[...])
    k = _group_rms(proj[:, QK_W:2 * QK_W], gmat, kg_ref[...])
    if rope:
        cos = cos_ref[...]
        sin = sin_ref[...]
        q = _rope(q, cos, sin)
        k = _rope(k, cos, sin)
    v = proj[:, 2 * QK_W:3 * QK_W]

    qT = q.T
    row = lax.broadcasted_iota(jnp.int32, qT.shape, 0)
    comp0 = (row % HEAD_DIM) < (HEAD_DIM // 2)
    qT_ref[0, 0] = jnp.where(comp0, qT, 0.0).astype(BF16)
    qT_ref[0, 1] = jnp.where(comp0, 0.0, qT).astype(BF16)
    k_ref[0] = k.astype(BF16)
    vT_ref[0, 0] = v.T.astype(BF16)
    u_ref[0] = proj[:, 3 * QK_W:]


def _project(x, modp, g1, w_in, gmat, qg, kg, cos, sin, *, tile):
    b, l, d = x.shape
    e = w_in.shape[1]
    nt = l // tile
    rope = cos is not None
    const = lambda bb, i: (0, 0)
    in_specs = [
        pl.BlockSpec((1, tile, d), lambda bb, i: (bb, i, 0)),
        pl.BlockSpec((1, 8, d), lambda bb, i: (bb, 0, 0)),
        pl.BlockSpec((1, d), const),
        pl.BlockSpec((d, e), const, pipeline_mode=pl.Buffered(1)),
        pl.BlockSpec((QK_W, QK_W), const, pipeline_mode=pl.Buffered(1)),
        pl.BlockSpec((1, QK_W), const),
        pl.BlockSpec((1, QK_W), const),
    ]
    args = [x, modp, g1, w_in, gmat, qg, kg]
    if rope:
        in_specs += [pl.BlockSpec((tile, V_DIM), lambda bb, i: (i, 0)),
                     pl.BlockSpec((tile, V_DIM), lambda bb, i: (i, 0))]
        args += [cos, sin]
    side = e - 3 * QK_W
    out_shape = (
        jax.ShapeDtypeStruct((b, 2, QK_W, l), BF16),
        jax.ShapeDtypeStruct((b, l, QK_W), BF16),
        jax.ShapeDtypeStruct((b, nt, QK_W, tile), BF16),
        jax.ShapeDtypeStruct((b, l, side), F32),
    )
    out_specs = (
        pl.BlockSpec((1, 2, QK_W, tile), lambda bb, i: (bb, 0, 0, i)),
        pl.BlockSpec((1, tile, QK_W), lambda bb, i: (bb, i, 0)),
        pl.BlockSpec((1, 1, QK_W, tile), lambda bb, i: (bb, i, 0, 0)),
        pl.BlockSpec((1, tile, side), lambda bb, i: (bb, i, 0)),
    )
    est = (d * e * 2 + 2 * tile * d * 4 + 3 * tile * e * 4 + 2 * tile * side * 4
           + 8 * tile * QK_W * 4)
    return pl.pallas_call(
        functools.partial(_proj_kernel, rope=rope),
        grid=(b, nt),
        in_specs=in_specs,
        out_specs=out_specs,
        out_shape=out_shape,
        compiler_params=pltpu.CompilerParams(
            dimension_semantics=("arbitrary", "arbitrary"),
            vmem_limit_bytes=_vmem_limit(est)),
        name="proj_rope" if rope else "proj_ctx",
    )(*args)


def _attn_kernel(*refs, n_lat_blocks, kblk, one_minus_init):
    if n_lat_blocks:
        (lam_ref, flag_ref, qT_ref, kc_ref, vTc_ref, k_ref, vT_ref, g_ref, o_ref,
         m_sc, l_sc, acc_sc, s_sc, mb_sc, p_sc, al_sc) = refs
    else:
        lam_ref, flag_ref, qT_ref, kc_ref, vTc_ref, g_ref, o_ref, m_sc, l_sc, acc_sc = refs

    def context_block(stabilise):
        kb = kc_ref[0]
        vb = vTc_ref[0, 0]
        for c in range(2):
            s = jnp.dot(kb, qT_ref[0, c], preferred_element_type=F32)
            if stabilise:
                m = jnp.max(s, axis=0, keepdims=True)
                m_sc[c] = m
                s = s - m
            p = jnp.exp2(s)
            l_sc[c] = jnp.sum(p, axis=0, keepdims=True)
            acc_sc[c] = jnp.dot(vb, p.astype(BF16), preferred_element_type=F32)

    def bounded_path():
        context_block(False)
        if not n_lat_blocks:
            return

        def probs(j, slot):
            start = pl.multiple_of(j * kblk, kblk)
            kj = k_ref[0, pl.ds(start, kblk), :]
            for c in range(2):
                p = jnp.exp2(jnp.dot(kj, qT_ref[0, c], preferred_element_type=F32))
                l_sc[c] = l_sc[c] + jnp.sum(p, axis=0, keepdims=True)
                p_sc[slot, c] = p.astype(BF16)

        def values(j, slot):
            vj = vT_ref[0, j]
            for c in range(2):
                acc_sc[c] = acc_sc[c] + jnp.dot(vj, p_sc[slot, c],
                                                preferred_element_type=F32)

        n = n_lat_blocks
        probs(0, 0)
        for j in range(1, n):
            probs(j, j % 2)
            values(j - 1, (j - 1) % 2)
        values(n - 1, (n - 1) % 2)

    def online_path():
        context_block(True)
        if not n_lat_blocks:
            return

        def scores(j, slot):
            start = pl.multiple_of(j * kblk, kblk)
            kj = k_ref[0, pl.ds(start, kblk), :]
            for c in range(2):
                s = jnp.dot(kj, qT_ref[0, c], preferred_element_type=F32)
                s_sc[slot, c] = s
                mb_sc[slot, c] = jnp.max(s, axis=0, keepdims=True)

        def softmax(slot):
            for c in range(2):
                m_old = m_sc[c]
                m_new = jnp.maximum(m_old, mb_sc[slot, c])
                alpha = jnp.exp2(m_old - m_new)
                p = jnp.exp2(s_sc[slot, c] - m_new)
                l_sc[c] = alpha * l_sc[c] + jnp.sum(p, axis=0, keepdims=True)
                p_sc[slot, c] = p.astype(BF16)
                al_sc[slot, c] = alpha
                m_sc[c] = m_new

        def values(j, slot):
            vj = vT_ref[0, j]
            for c in range(2):
                acc_sc[c] = (al_sc[slot, c] * acc_sc[c]
                             + jnp.dot(vj, p_sc[slot, c], preferred_element_type=F32))

        n = n_lat_blocks
        scores(0, 0)
        scores(1, 1)
        softmax(0)

        def body(jj, carry):
            j = 2 * jj + 1
            scores(j + 1, 0)
            values(j - 1, 0)
            softmax(1)
            scores(j + 2, 1)
            values(j, 1)
            softmax(0)
            return carry
        lax.fori_loop(0, (n - 2) // 2, body, 0)
        values(n - 2, 0)
        softmax(1)
        values(n - 1, 1)

    lax.cond(flag_ref[0] != 0, bounded_path, online_path)

    lam = lam_ref[0]
    o = acc_sc[0] * (1.0 / l_sc[0]) - lam * (acc_sc[1] * (1.0 / l_sc[1]))
    ms = jnp.mean(o * o, axis=0, keepdims=True)
    o = o * lax.rsqrt(ms + EPS) * (g_ref[...] * one_minus_init)
    o_ref[0] = o.T.astype(BF16)


def _attention(lam, bounded, qT, kc, vTc, k, vT, g_col, *, qtile, one_minus_init):
    b, _, _, lq = qT.shape
    lc = kc.shape[1]
    nq = lq // qtile
    in_specs = [
        pl.BlockSpec(memory_space=pltpu.SMEM),
        pl.BlockSpec(memory_space=pltpu.SMEM),
        pl.BlockSpec((1, 2, V_DIM, qtile), lambda bb, h, i: (bb, 0, h, i)),
        pl.BlockSpec((1, lc, V_DIM), lambda bb, h, i: (bb, 0, h)),
        pl.BlockSpec((1, 1, V_DIM, lc), lambda bb, h, i: (bb, 0, h, 0)),
    ]
    args = [lam, bounded, qT, kc, vTc]
    n_lat, kblk = 0, 0
    est = 4 * lc * V_DIM * 2
    if k is not None:
        lk = k.shape[1]
        n_lat, kblk = vT.shape[1], vT.shape[3]
        in_specs += [
            pl.BlockSpec((1, lk, V_DIM), lambda bb, h, i: (bb, 0, h)),
            pl.BlockSpec((1, n_lat, V_DIM, kblk), lambda bb, h, i: (bb, 0, h, 0)),
        ]
        args += [k, vT]
        est += 4 * lk * V_DIM * 2
    in_specs.append(pl.BlockSpec((V_DIM, 1), lambda bb, h, i: (0, 0)))
    args.append(g_col)
    est += 6 * max(kblk, lc) * qtile * 4 + 4 * V_DIM * qtile * 4
    scratch = [
        pltpu.VMEM((2, 1, qtile), F32),
        pltpu.VMEM((2, 1, qtile), F32),
        pltpu.VMEM((2, V_DIM, qtile), F32),
    ]
    if n_lat:
        assert n_lat >= 2 and n_lat % 2 == 0
        scratch += [
            pltpu.VMEM((2, 2, kblk, qtile), F32),
            pltpu.VMEM((2, 2, 1, qtile), F32),
            pltpu.VMEM((2, 2, kblk, qtile), BF16),
            pltpu.VMEM((2, 2, 1, qtile), F32),
        ]
    return pl.pallas_call(
        functools.partial(_attn_kernel, n_lat_blocks=n_lat, kblk=kblk,
                          one_minus_init=one_minus_init),
        grid=(b, HEADS, nq),
        in_specs=in_specs,
        out_specs=pl.BlockSpec((1, qtile, V_DIM), lambda bb, h, i: (bb, i, h)),
        out_shape=jax.ShapeDtypeStruct((b, lq, QK_W), BF16),
        scratch_shapes=scratch,
        compiler_params=pltpu.CompilerParams(
            dimension_semantics=("arbitrary", "arbitrary", "arbitrary"),
            vmem_limit_bytes=_vmem_limit(est)),
        name="diff_attn" if n_lat else "diff_attn_ctx",
    )(*args)


def _mix_kernel(prev_ref, cur_ref, next_ref, pw_ref, ps_ref, dw_ref, db_ref, lg_ref, lb_ref,
                o_ref, up_sc, glu_sc, sh_sc, s2_sc, s4_sc, s8_sc, *, tile, seq_len):
    i = pl.program_id(1)
    n = pl.num_programs(1)
    w = MIX_SIDE_W
    ext = tile + 2 * HALO

    def put(rows, blk, valid):
        up = blk[:, 0:w]
        glu = blk[:, w:2 * w] * jax.nn.sigmoid(blk[:, 2 * w:3 * w])
        if valid is not None:
            up = jnp.where(valid, up, 0.0)
            glu = jnp.where(valid, glu, 0.0)
        up_sc[rows, :] = up
        glu_sc[rows, :] = glu

    put(slice(0, HALO), prev_ref[0], i > 0)
    put(slice(HALO, HALO + tile), cur_ref[0], None)
    put(slice(HALO + tile, ext), next_ref[0], i < n - 1)

    span = tile + CONV_SPAN_PAD
    for r in range(SUBLANES - 1):
        sh_sc[r, 0:span, :] = glu_sc[pl.ds(r + 1, span), :]
    for c0 in range(0, tile, CONV_ROW_CHUNK):
        acc = jnp.zeros((CONV_ROW_CHUNK, w), F32)
        for kk in range(CONV_K):
            a, r = divmod(HALO - CONV_PAD + kk, SUBLANES)
            lo = a * SUBLANES + c0
            if r == 0:
                src = glu_sc[lo:lo + CONV_ROW_CHUNK, :]
            else:
                src = sh_sc[r - 1, lo:lo + CONV_ROW_CHUNK, :]
            acc = acc + src * dw_ref[kk:kk + 1, :]
        y = acc + db_ref[...]
        mu = jnp.mean(y, axis=-1, keepdims=True)
        yc = y - mu
        var = jnp.mean(yc * yc, axis=-1, keepdims=True)
        y = yc * lax.rsqrt(var + EPS) * lg_ref[...] + lb_ref[...]
        o_ref[0, c0:c0 + CONV_ROW_CHUNK, w:2 * w] = (y * jax.nn.sigmoid(y)).astype(BF16)

    lv = V7X_LANES
    zeros = jnp.zeros((SUBLANES, w), F32)
    for sc in (s2_sc, s4_sc):
        sc[0:SUBLANES, :] = zeros
        sc[ext - SUBLANES:ext, :] = zeros
    s8_sc[0:SUBLANES, :] = zeros[:, 0:lv]
    s8_sc[ext - SUBLANES:ext, :] = zeros[:, 0:lv]
    inner = ext - 2 * SUBLANES
    body = slice(SUBLANES, ext - SUBLANES)
    s2_sc[body, :] = up_sc[pl.ds(SUBLANES - 1, inner), :] + up_sc[body, :]
    s4_sc[body, :] = s2_sc[pl.ds(SUBLANES - 1, inner), :] + s2_sc[pl.ds(SUBLANES + 1, inner), :]
    s8_sc[body, :] = (s4_sc[pl.ds(SUBLANES - 2, inner), lv:2 * lv]
                      + s4_sc[pl.ds(SUBLANES + 2, inner), lv:2 * lv])
    s16 = s8_sc[pl.ds(HALO - 4, tile), :] + s8_sc[pl.ds(HALO + 4, tile), :]
    first = lax.broadcasted_iota(jnp.int32, (tile, lv), 1) < POOL_GROUP
    centre = slice(HALO, HALO + tile)
    wsum = jnp.concatenate(
        [jnp.where(first, s2_sc[centre, 0:lv], s4_sc[centre, 0:lv]),
         jnp.where(first, s8_sc[centre, :], s16)], axis=1)

    t = (i * tile + lax.broadcasted_iota(jnp.int32, (tile, w), 0))
    lane = lax.broadcasted_iota(jnp.int32, (tile, w), 1)
    half = jnp.left_shift(1, lane // POOL_GROUP)
    cnt = (jnp.minimum(t + half, seq_len) - jnp.maximum(t - half, 0)).astype(F32)
    p = wsum / cnt - up_sc[centre, :]
    pool_o = jnp.dot(p.astype(BF16), pw_ref[...], preferred_element_type=F32) * ps_ref[...]
    o_ref[0, :, 0:w] = pool_o.astype(BF16)


def _mixers(u, pw, ps, dw, db, lg, lb, *, tile):
    b, l, side = u.shape
    nt = l // tile
    r = tile // HALO
    nh = l // HALO
    w = MIX_SIDE_W
    ext = tile + 2 * HALO
    assert POOL_WINDOWS == (2, 4, 8, 16) and tile % CONV_ROW_CHUNK == 0
    const = lambda bb, i: (0, 0)
    in_specs = [
        pl.BlockSpec((1, HALO, side), lambda bb, i: (bb, jnp.maximum(i * r - 1, 0), 0)),
        pl.BlockSpec((1, tile, side), lambda bb, i: (bb, i, 0)),
        pl.BlockSpec((1, HALO, side), lambda bb, i: (bb, jnp.minimum((i + 1) * r, nh - 1), 0)),
        pl.BlockSpec((w, w), const),
        pl.BlockSpec((1, w), const),
        pl.BlockSpec((CONV_K, w), const),
        pl.BlockSpec((1, w), const),
        pl.BlockSpec((1, w), const),
        pl.BlockSpec((1, w), const),
    ]
    est = 2 * tile * side * 4 + (SUBLANES + 5) * ext * w * 4 + 8 * tile * w * 4
    return pl.pallas_call(
        functools.partial(_mix_kernel, tile=tile, seq_len=l),
        grid=(b, nt),
        in_specs=in_specs,
        out_specs=pl.BlockSpec((1, tile, 2 * w), lambda bb, i: (bb, i, 0)),
        out_shape=jax.ShapeDtypeStruct((b, l, 2 * w), BF16),
        scratch_shapes=[pltpu.VMEM((ext, w), F32),
                        pltpu.VMEM((ext, w), F32),
                        pltpu.VMEM((SUBLANES - 1, tile + CONV_SPAN_PAD, w), F32),
                        pltpu.VMEM((ext, w), F32),
                        pltpu.VMEM((ext, w), F32),
                        pltpu.VMEM((ext, V7X_LANES), F32)],
        compiler_params=pltpu.CompilerParams(
            dimension_semantics=("arbitrary", "arbitrary"),
            vmem_limit_bytes=_vmem_limit(est)),
        name="pool_conv",
    )(u, u, u, pw, ps, dw, db, lg, lb)


def _out_ffn_kernel(x_ref, a_ref, s_ref, mod_ref, wo_ref, n2_ref, wi_ref, wd_ref, o_ref,
                    *, hidden, bounds):
    x = x_ref[0]
    mix = jnp.concatenate([a_ref[0], s_ref[0]], axis=1)
    x1 = x + mod_ref[0, 2:3, :] * jnp.dot(mix, wo_ref[...], preferred_element_type=F32)
    ms = jnp.mean(x1 * x1, axis=-1, keepdims=True)
    h = x1 * lax.rsqrt(ms + EPS) * n2_ref[...]
    h = (h * (1.0 + mod_ref[0, 4:5, :]) + mod_ref[0, 3:4, :]).astype(BF16)
    acc = jnp.zeros(x.shape, F32)
    for lo, hi in zip(bounds[:-1], bounds[1:]):
        g = jnp.dot(h, wi_ref[:, lo:hi], preferred_element_type=F32)
        u = jnp.dot(h, wi_ref[:, hidden + lo:hidden + hi], preferred_element_type=F32)
        act = (g * jax.nn.sigmoid(g) * u).astype(BF16)
        acc = acc + jnp.dot(act, wd_ref[lo:hi, :], preferred_element_type=F32)
    o_ref[0] = x1 + mod_ref[0, 5:6, :] * acc


def _out_ffn(x, attn, side, modp, w_out, n2, w_ffn_in, w_ffn_out, *, tile):
    b, l, d = x.shape
    hidden = w_ffn_out.shape[0]
    assert hidden % V7X_MXU_WIDTH == 0
    mid = (hidden // V7X_MXU_WIDTH + 1) // 2 * V7X_MXU_WIDTH
    bounds = (0, mid, hidden)
    const = lambda bb, i: (0, 0)
    in_specs = [
        pl.BlockSpec((1, tile, d), lambda bb, i: (bb, i, 0)),
        pl.BlockSpec((1, tile, attn.shape[2]), lambda bb, i: (bb, i, 0)),
        pl.BlockSpec((1, tile, side.shape[2]), lambda bb, i: (bb, i, 0)),
        pl.BlockSpec((1, 8, d), lambda bb, i: (bb, 0, 0)),
        pl.BlockSpec(w_out.shape, const, pipeline_mode=pl.Buffered(1)),
        pl.BlockSpec((1, d), const),
        pl.BlockSpec(w_ffn_in.shape, const, pipeline_mode=pl.Buffered(1)),
        pl.BlockSpec(w_ffn_out.shape, const, pipeline_mode=pl.Buffered(1)),
    ]
    est = ((w_out.size + w_ffn_in.size + w_ffn_out.size) * 2 + 4 * tile * d * 4
           + 2 * tile * d * 2 + 3 * tile * mid * 4 + 3 * tile * d * 4)
    return pl.pallas_call(
        functools.partial(_out_ffn_kernel, hidden=hidden, bounds=bounds),
        grid=(b, l // tile),
        in_specs=in_specs,
        out_specs=pl.BlockSpec((1, tile, d), lambda bb, i: (bb, i, 0)),
        out_shape=jax.ShapeDtypeStruct((b, l, d), F32),
        compiler_params=pltpu.CompilerParams(
            dimension_semantics=("arbitrary", "arbitrary"),
            vmem_limit_bytes=_vmem_limit(est)),
        name="out_ffn",
    )(x, attn, side, modp, w_out, n2, w_ffn_in, w_ffn_out)


def _head_lane_order():
    comp = np.zeros(V_DIM, np.int32)
    dim = np.zeros(V_DIM, np.int32)
    for j in range(V_DIM):
        second = j // HEAD_DIM
        c = (j % HEAD_DIM) // (HEAD_DIM // 2)
        p = j % (HEAD_DIM // 2)
        comp[j] = c
        dim[j] = 2 * p + second
    return comp, dim


def _qk_column_perm():
    comp, dim = _head_lane_order()
    cols = [h * V_DIM + comp[j] * HEAD_DIM + dim[j] for h in range(HEADS) for j in range(V_DIM)]
    return np.asarray(cols, np.int32)


def _group_matrix():
    comp, _ = _head_lane_order()
    head = np.arange(QK_W) // V_DIM
    c = np.tile(comp, HEADS)
    same = (head[:, None] == head[None, :]) & (c[:, None] == c[None, :])
    return jnp.asarray(same.astype(np.float32) / HEAD_DIM, BF16)


def _rope_lane_tables(seq_len):
    rows = seq_len // GRID_W
    row = jnp.repeat(jnp.arange(rows), GRID_W).astype(F32)
    col = jnp.tile(jnp.arange(GRID_W), rows).astype(F32)
    half = HEAD_DIM // 2
    inv_freq = ROPE_THETA ** (-jnp.arange(0, half, 2, dtype=F32) / half)
    ang = jnp.concatenate([row[:, None] * inv_freq, col[:, None] * inv_freq], axis=-1)
    cos, sin = jnp.cos(ang), jnp.sin(ang)
    cos_l = jnp.tile(cos, (1, 4))
    sin_l = jnp.concatenate([-sin, -sin, sin, sin], axis=1)
    return cos_l, sin_l


def _mod_rows(mod_l, b, d):
    m = mod_l.reshape(8, 6, d)
    pad = jnp.zeros((b, 2, d), F32)
    lat = jnp.concatenate([m[:b], pad], axis=1)
    ctx = jnp.concatenate([jnp.broadcast_to(m[b:b + 1], (b, 6, d)), pad], axis=1)
    return lat, ctx


def kernel(x, c, ctx, c_ctx, w_mod, b_mod, norm1_g, w_in, q_norm_g, k_norm_g, lambda_q1,
           lambda_k1, lambda_q2, lambda_k2, subln_g, pool_w, pool_scale, conv_dw_w, conv_dw_b,
           conv_ln_g, conv_ln_b, w_out, norm2_g, w_ffn_in, w_ffn_out):
    b, l, d = x.shape
    lc = ctx.shape[1]
    depth = w_mod.shape[0]
    assert b + 1 <= 8 and l % TOKEN_TILE == 0 and l % Q_TILE == 0 and lc % HALO == 0

    cond8 = jnp.concatenate([c, c_ctx[None, :], jnp.zeros((8 - b - 1, d), F32)], axis=0)
    mod_all = _modulation(cond8, w_mod, b_mod)

    perm = _qk_column_perm()
    _, lane_dim = _head_lane_order()
    lane_dim = np.tile(lane_dim, HEADS)
    w_q = w_in[:, :, 0:QK_W][:, :, perm]
    w_k = w_in[:, :, QK_W:2 * QK_W][:, :, perm]
    w_in_b = jnp.concatenate([w_q, w_k, w_in[:, :, 2 * QK_W:]], axis=2).astype(BF16)
    w_out_b = w_out.astype(BF16)
    w_ffn_in_b = w_ffn_in.astype(BF16)
    w_ffn_out_b = w_ffn_out.astype(BF16)
    qg_lane = q_norm_g[:, lane_dim] * (HEAD_DIM ** -0.5 * math.log2(math.e))
    kg_lane = k_norm_g[:, lane_dim]
    gmat = _group_matrix()
    cos_l, sin_l = _rope_lane_tables(l)
    pw_bd = jax.vmap(lambda w4: jax.scipy.linalg.block_diag(*[w4[g] for g in range(4)]))(
        pool_w).astype(BF16)

    xc = ctx
    for li in range(depth):
        last = li == depth - 1
        lambda_init = 0.8 - 0.6 * math.exp(-0.3 * li)
        lam = (jnp.exp(jnp.sum(lambda_q1[li] * lambda_k1[li]).astype(F32))
               - jnp.exp(jnp.sum(lambda_q2[li] * lambda_k2[li]).astype(F32))
               + lambda_init).reshape(1)
        mod_lat, mod_ctx = _mod_rows(mod_all[li], b, d)
        g1 = norm1_g[li][None, :]
        n2 = norm2_g[li][None, :]
        qg = qg_lane[li][None, :]
        kg = kg_lane[li][None, :]
        sg = subln_g[li][:, None]
        score_bound = (HEAD_DIM * SCORE_BOUND_MARGIN * jnp.max(jnp.abs(qg_lane[li]))
                       * jnp.max(jnp.abs(kg_lane[li])))
        bounded = (score_bound <= SCORE_BOUND_LOG2).astype(jnp.int32).reshape(1)
        mix_params = (pw_bd[li], pool_scale[li][None, :], conv_dw_w[li],
                      conv_dw_b[li][None, :], conv_ln_g[li][None, :], conv_ln_b[li][None, :])

        qT, k, vT, u = _project(x, mod_lat, g1, w_in_b[li], gmat, qg, kg, cos_l, sin_l,
                                tile=TOKEN_TILE)
        qTc, kc, vTc, uc = _project(xc, mod_ctx, g1, w_in_b[li], gmat, qg, kg, None, None,
                                    tile=lc)
        attn = _attention(lam, bounded, qT, kc, vTc, k, vT, sg, qtile=Q_TILE,
                          one_minus_init=1.0 - lambda_init)
        side = _mixers(u, *mix_params, tile=TOKEN_TILE)
        x = _out_ffn(x, attn, side, mod_lat, w_out_b[li], n2, w_ffn_in_b[li], w_ffn_out_b[li],
                     tile=TOKEN_TILE)
        if not last:
            attn_c = _attention(lam, bounded, qTc, kc, vTc, None, None, sg, qtile=lc,
                                one_minus_init=1.0 - lambda_init)
            side_c = _mixers(uc, *mix_params, tile=lc)
            xc = _out_ffn(xc, attn_c, side_c, mod_ctx, w_out_b[li], n2, w_ffn_in_b[li],
                          w_ffn_out_b[li], tile=lc)
    return x
```
